```python
import jax, jax.numpy as jnp
from jax import lax
import numpy as np

D_MODEL = 1024
BATCH = 32
SEQ = 256
DEPTH = 4
DEC_BATCH = 2
DEC_SEQ = 1024
PAST_LEN = 512

GRID_W = 64
H_A = 8
KV_A = 2
G_A = H_A // KV_A
HD_A = 64
WINDOW = 128
BLK = WINDOW
C_B = 512
CONV_W = 31
H_C = 8
Q_LORA = 384
KV_LORA = 256
NOPE_C = 64
ROPE_C = 32
V_C = 64
N_EXP = 32
TOP_K = 4
D_FF = 1024
SWIGLU_LIMIT = 7.0
SWIGLU_ALPHA = 1.702
N_BRANCH = 3
ROPE_BASE = 10000.0
EPS = 1e-6
QBLK = 128
NEG = -1e30
SCALE_A = HD_A ** -0.5
SCALE_C = (NOPE_C + ROPE_C) ** -0.5
OFF_AK = H_A * HD_A
OFF_AV = OFF_AK + KV_A * HD_A
OFF_B = OFF_AV + KV_A * HD_A
OFF_CQ = OFF_B + 2 * C_B
OFF_CKV = OFF_CQ + Q_LORA
OFF_GATE = OFF_CKV + KV_LORA + ROPE_C
IN_COLS = OFF_GATE + N_BRANCH * D_MODEL
SPLITS = (OFF_AK, OFF_AV, OFF_B, OFF_CQ, OFF_CKV, OFF_GATE)

kernel_name = "hybrid_prefix_diffusion_swa_conformer_mla_moe"


def rmsnorm(x, g):
    xf = x.astype(jnp.float32)
    y = xf * lax.rsqrt(jnp.mean(xf * xf, axis=-1, keepdims=True) + EPS)
    return (y * g.astype(jnp.float32)).astype(x.dtype)


def layernorm(x, g, b):
    xf = x.astype(jnp.float32)
    mu = jnp.mean(xf, axis=-1, keepdims=True)
    var = jnp.mean(jnp.square(xf - mu), axis=-1, keepdims=True)
    y = (xf - mu) * lax.rsqrt(var + EPS)
    return (y * g.astype(jnp.float32) + b.astype(jnp.float32)).astype(x.dtype)


def _rope_1d(x, pos):
    d = x.shape[-1]
    inv = ROPE_BASE ** (-jnp.arange(0, d, 2, dtype=jnp.float32) / d)
    ang = pos[:, None] * inv[None, :]
    cos = jnp.cos(ang)[None, :, None, :].astype(x.dtype)
    sin = jnp.sin(ang)[None, :, None, :].astype(x.dtype)
    x1, x2 = jnp.split(x, 2, axis=-1)
    return jnp.concatenate([x1 * cos - x2 * sin, x2 * cos + x1 * sin], axis=-1)


def rope_2d(x):
    T = x.shape[1]
    rows = T // GRID_W
    row = jnp.repeat(jnp.arange(rows), GRID_W).astype(jnp.float32)
    col = jnp.tile(jnp.arange(GRID_W), rows).astype(jnp.float32)
    xr, xc = jnp.split(x, 2, axis=-1)
    return jnp.concatenate([_rope_1d(xr, row), _rope_1d(xc, col)], axis=-1)


def attend_dense(q, k, v, sink, scale):
    B, T, Hk, G, d = q.shape
    nb = T // QBLK
    qb = jnp.moveaxis(q.reshape(B, nb, QBLK, Hk, G, d), 1, 0)

    def one(qi):
        s = jnp.einsum('bqhgd,bshd->bhgqs', qi, k, preferred_element_type=jnp.float32) * scale
        if sink is not None:
            sk = jnp.broadcast_to(sink.astype(jnp.float32)[None, :, :, None, None], s.shape[:-1] + (1,))
            p = jax.nn.softmax(jnp.concatenate([s, sk], axis=-1), axis=-1)[..., :-1]
        else:
            p = jax.nn.softmax(s, axis=-1)
        return jnp.einsum('bhgqs,bshd->bqhgd', p.astype(v.dtype), v)

    o = lax.map(one, qb)
    return jnp.moveaxis(o, 0, 1).reshape(B, T, Hk, G, v.shape[-1])


def attend_window(q, k, v, k_ctx, v_ctx, sink, scale):
    B, T, Hk, G, d = q.shape
    nb = T // BLK
    qb = q.reshape(B, nb, BLK, Hk, G, d)

    def bands(a):
        ap = jnp.pad(a, ((0, 0), (BLK, BLK), (0, 0), (0, 0))).reshape(B, nb + 2, BLK, Hk, a.shape[-1])
        return jnp.concatenate([ap[:, :-2], ap[:, 1:-1], ap[:, 2:]], axis=2)

    kb, vb = bands(k), bands(v)
    qpos = jnp.arange(T).reshape(nb, BLK)
    kpos = jnp.arange(nb)[:, None] * BLK - BLK + jnp.arange(3 * BLK)[None, :]
    valid = ((jnp.abs(qpos[:, :, None] - kpos[:, None, :]) <= WINDOW)
             & (kpos[:, None, :] >= 0) & (kpos[:, None, :] < T))
    s_loc = jnp.einsum('bnqhgd,bnkhd->bnhgqk', qb, kb, preferred_element_type=jnp.float32) * scale
    s_loc = jnp.where(valid[None, :, None, None], s_loc, NEG)
    s_ctx = jnp.einsum('bnqhgd,bshd->bnhgqs', qb, k_ctx, preferred_element_type=jnp.float32) * scale
    s_snk = jnp.broadcast_to(sink.astype(jnp.float32)[None, None, :, :, None, None], s_loc.shape[:-1] + (1,))
    p = jax.nn.softmax(jnp.concatenate([s_loc, s_ctx, s_snk], axis=-1), axis=-1).astype(v.dtype)
    p_loc, p_ctx = p[..., :3 * BLK], p[..., 3 * BLK:-1]
    o = (jnp.einsum('bnhgqk,bnkhd->bnqhgd', p_loc, vb)
         + jnp.einsum('bnhgqs,bshd->bnqhgd', p_ctx, v_ctx))
    return o.reshape(B, T, Hk, G, d)


def mla_project(qc, kvc, qa_g, qb_w, kv_g, rotate):
    B, T, _ = qc.shape
    q = (rmsnorm(qc, qa_g) @ qb_w).reshape(B, T, H_C, NOPE_C + ROPE_C)
    q_nope, q_rope = q[..., :NOPE_C], q[..., NOPE_C:]
    ckv = rmsnorm(kvc[..., :KV_LORA], kv_g)
    krope = kvc[..., KV_LORA:]
    if rotate:
        q_rope = rope_2d(q_rope)
        krope = rope_2d(krope[:, :, None, :])[:, :, 0, :]
    return jnp.concatenate([q_nope, q_rope], axis=-1), ckv, krope


def mla_expand(ckv, krope, kvb_w):
    B, S, _ = ckv.shape
    kv = (ckv @ kvb_w).reshape(B, S, H_C, NOPE_C + V_C)
    k = jnp.concatenate([kv[..., :NOPE_C],
                         jnp.broadcast_to(krope[:, :, None, :], (B, S, H_C, ROPE_C))], axis=-1)
    return k, kv[..., NOPE_C:]


def conformer_conv(u, dw_w, dw_b, ln_g, ln_b, w_pw):
    a, b = jnp.split(u, 2, axis=-1)
    g = a * jax.nn.sigmoid(b)
    y = lax.conv_general_dilated(g, dw_w[:, None, :].astype(g.dtype), (1,),
                                 [(CONV_W // 2, CONV_W // 2)],
                                 dimension_numbers=('NWC', 'WIO', 'NWC'),
                                 feature_group_count=C_B) + dw_b
    y = layernorm(y, ln_g, ln_b)
    return jax.nn.silu(y) @ w_pw


def moe(h, router_w, router_b, w1, b1, w2, b2):
    B, T, D = h.shape
    xt = h.reshape(B * T, D)
    logits = (xt @ router_w + router_b).astype(jnp.float32)
    top_v, top_i = lax.top_k(logits, TOP_K)
    top_p = jax.nn.softmax(top_v, axis=-1)
    gate = jnp.sum(jax.nn.one_hot(top_i, N_EXP, dtype=jnp.float32) * top_p[..., None], axis=1)

    def expert(acc, e):
        w1e, b1e, w2e, b2e, ge = e
        u = xt @ w1e + b1e
        gl = jnp.minimum(u[:, :D_FF], SWIGLU_LIMIT)
        up = jnp.clip(u[:, D_FF:], -SWIGLU_LIMIT, SWIGLU_LIMIT)
        y = ((up + 1.0) * (gl * jax.nn.sigmoid(SWIGLU_ALPHA * gl))) @ w2e + b2e
        return acc + ge[:, None].astype(y.dtype) * y, None

    acc, _ = lax.scan(expert, jnp.zeros_like(xt), (w1, b1, w2, b2, gate.T))
    return acc.reshape(B, T, D)


def trunk_layer(x, cond, p, ctx=None):
    B, T, D = x.shape
    mod = (jax.nn.silu(cond) @ p['mod_w'] + p['mod_b'])[:, None, :]
    sh1, sc1, g1, sh2, sc2, g2 = jnp.split(mod, 6, axis=-1)
    h = rmsnorm(x, p['norm1_g']) * (1.0 + sc1) + sh1
    qa, ka, va, ub, qc, kvc, gl = jnp.split(h @ p['w_in'], list(SPLITS), axis=-1)
    qa = qa.reshape(B, T, H_A, HD_A)
    ka = ka.reshape(B, T, KV_A, HD_A)
    va = va.reshape(B, T, KV_A, HD_A)
    sink = p['swa_sink'].reshape(KV_A, G_A)
    is_ctx = ctx is None
    q_c, ckv, krope = mla_project(qc, kvc, p['mla_qa_g'], p['mla_qb_w'], p['mla_kv_g'], rotate=not is_ctx)
    k_c, v_c = mla_expand(ckv, krope, p['mla_kvb_w'])
    if is_ctx:
        o_a = attend_dense(qa.reshape(B, T, KV_A, G_A, HD_A), ka, va, sink, SCALE_A)
        o_c = attend_dense(q_c[:, :, :, None, :], k_c, v_c, None, SCALE_C)
        new_ctx = (ka, va, ckv, krope)
    else:
        ka_ctx, va_ctx, ckv_ctx, krope_ctx = ctx
        q_r = rope_2d(qa).reshape(B, T, KV_A, G_A, HD_A)
        o_a = attend_window(q_r, rope_2d(ka), va, ka_ctx, va_ctx, sink, SCALE_A)
        kc_ctx, vc_ctx = mla_expand(ckv_ctx, krope_ctx, p['mla_kvb_w'])
        o_c = attend_dense(q_c[:, :, :, None, :],
                           jnp.concatenate([kc_ctx, k_c], axis=1),
                           jnp.concatenate([vc_ctx, v_c], axis=1), None, SCALE_C)
        new_ctx = None
    br_a = o_a.reshape(B, T, H_A * HD_A) @ p['w_br_a']
    br_b = conformer_conv(ub, p['conv_dw_w'], p['conv_dw_b'], p['conv_ln_g'], p['conv_ln_b'], p['w_br_b'])
    br_c = o_c.reshape(B, T, H_C * V_C) @ p['w_br_c']
    gates = jax.nn.sigmoid(gl).reshape(B, T, N_BRANCH, D)
    merged = gates[:, :, 0] * br_a + gates[:, :, 1] * br_b + gates[:, :, 2] * br_c
    x = x + g1 * (merged @ p['w_out'])
    h2 = rmsnorm(x, p['norm2_g']) * (1.0 + sc2) + sh2
    x = x + g2 * moe(h2, p['router_w'], p['router_b'], p['exp_w1'], p['exp_b1'], p['exp_w2'], p['exp_b2'])
    return x, new_ctx


def setup_inputs(seed: int = 0) -> dict:
    key = jax.random.key(seed)
    ks = list(jax.random.split(key, 40))

    def nrm(i, shape, scale):
        return jax.random.normal(ks[i], shape, jnp.float32) * scale

    L, D = DEPTH, D_MODEL
    return {
        'x_prompt': nrm(0, (BATCH, SEQ, D), 1.0),
        'x_sample': nrm(1, (DEC_BATCH, DEC_SEQ, D), 1.0),
        'cache_swa_k': nrm(2, (DEC_BATCH, L, PAST_LEN, KV_A, HD_A), 1.0),
        'cache_swa_v': nrm(3, (DEC_BATCH, L, PAST_LEN, KV_A, HD_A), 1.0),
        'cache_mla_ckv': nrm(4, (DEC_BATCH, L, PAST_LEN, KV_LORA), 1.0),
        'cache_mla_krope': nrm(5, (DEC_BATCH, L, PAST_LEN, ROPE_C), 1.0),
        'c': nrm(6, (DEC_BATCH, D), 1.0),
        'c_ctx': nrm(7, (D,), 1.0),
        'mod_w': nrm(8, (L, D, 6 * D), 0.5 * D ** -0.5),
        'mod_b': nrm(9, (L, 6 * D), 0.02),
        'norm1_g': 1.0 + nrm(10, (L, D), 0.05),
        'norm2_g': 1.0 + nrm(11, (L, D), 0.05),
        'w_in': nrm(12, (L, D, IN_COLS), D ** -0.5),
        'swa_sink': nrm(13, (L, H_A), 0.5),
        'conv_dw_w': nrm(14, (L, CONV_W, C_B), CONV_W ** -0.5),
        'conv_dw_b': nrm(15, (L, C_B), 0.02),
        'conv_ln_g': 1.0 + nrm(16, (L, C_B), 0.05),
        'conv_ln_b': nrm(17, (L, C_B), 0.02),
        'mla_qa_g': 1.0 + nrm(18, (L, Q_LORA), 0.05),
        'mla_qb_w': nrm(19, (L, Q_LORA, H_C * (NOPE_C + ROPE_C)), Q_LORA ** -0.5),
        'mla_kv_g': 1.0 + nrm(20, (L, KV_LORA), 0.05),
        'mla_kvb_w': nrm(21, (L, KV_LORA, H_C * (NOPE_C + V_C)), KV_LORA ** -0.5),
        'w_br_a': nrm(22, (L, H_A * HD_A, D), (H_A * HD_A) ** -0.5),
        'w_br_b': nrm(23, (L, C_B, D), C_B ** -0.5),
        'w_br_c': nrm(24, (L, H_C * V_C, D), (H_C * V_C) ** -0.5),
        'w_out': nrm(25, (L, D, D), D ** -0.5),
        'router_w': nrm(26, (L, D, N_EXP), D ** -0.5),
        'router_b': nrm(27, (L, N_EXP), 0.01),
        'exp_w1': nrm(28, (L, N_EXP, D, 2 * D_FF), D ** -0.5),
        'exp_b1': nrm(29, (L, N_EXP, 2 * D_FF), 0.02),
        'exp_w2': nrm(30, (L, N_EXP, D_FF, D), D_FF ** -0.5),
        'exp_b2': nrm(31, (L, N_EXP, D), 0.02),
        'final_g': 1.0 + nrm(32, (D,), 0.05),
    }


def reference(x_prompt, x_sample, cache_swa_k, cache_swa_v, cache_mla_ckv, cache_mla_krope, c, c_ctx,
              mod_w, mod_b, norm1_g, norm2_g, w_in, swa_sink, conv_dw_w, conv_dw_b, conv_ln_g, conv_ln_b,
              mla_qa_g, mla_qb_w, mla_kv_g, mla_kvb_w, w_br_a, w_br_b, w_br_c, w_out,
              router_w, router_b, exp_w1, exp_b1, exp_w2, exp_b2, final_g):
    def layer_params(l):
        return {
            'mod_w': mod_w[l], 'mod_b': mod_b[l], 'norm1_g': norm1_g[l], 'norm2_g': norm2_g[l],
            'w_in': w_in[l], 'swa_sink': swa_sink[l],
            'conv_dw_w': conv_dw_w[l], 'conv_dw_b': conv_dw_b[l], 'conv_ln_g': conv_ln_g[l], 'conv_ln_b': conv_ln_b[l],
            'mla_qa_g': mla_qa_g[l], 'mla_qb_w': mla_qb_w[l], 'mla_kv_g': mla_kv_g[l], 'mla_kvb_w': mla_kvb_w[l],
            'w_br_a': w_br_a[l], 'w_br_b': w_br_b[l], 'w_br_c': w_br_c[l], 'w_out': w_out[l],
            'router_w': router_w[l], 'router_b': router_b[l],
            'exp_w1': exp_w1[l], 'exp_b1': exp_b1[l], 'exp_w2': exp_w2[l], 'exp_b2': exp_b2[l],
        }

    xp = x_prompt
    ks_a, vs_a, ckvs, krs = [], [], [], []
    for l in range(DEPTH):
        xp, (k_a, v_a, ckv, kr) = trunk_layer(xp, c_ctx[None, :], layer_params(l))
        ks_a.append(k_a)
        vs_a.append(v_a)
        ckvs.append(ckv)
        krs.append(kr)
    y_prompt = rmsnorm(xp, final_g)
    new_swa_k = jnp.stack(ks_a, axis=1)
    new_swa_v = jnp.stack(vs_a, axis=1)
    new_mla_ckv = jnp.stack(ckvs, axis=1)
    new_mla_krope = jnp.stack(krs, axis=1)

    xs = x_sample
    for l in range(DEPTH):
        ctx = (cache_swa_k[:, l], cache_swa_v[:, l], cache_mla_ckv[:, l], cache_mla_krope[:, l])
        xs, _ = trunk_layer(xs, c, layer_params(l), ctx)
    y_sample = rmsnorm(xs, final_g)

    return (y_prompt, y_sample, new_swa_k, new_swa_v, new_mla_ckv, new_mla_krope)
```

```python
import functools

import jax
import jax.numpy as jnp
from jax import lax
from jax.experimental import pallas as pl
from jax.experimental.pallas import tpu as pltpu

F32 = jnp.float32
BF16 = jnp.bfloat16
I32 = jnp.int32

H_A, KV_A, HD_A, WINDOW = 8, 2, 64, 128
C_B, CONV_W = 512, 31
H_C, Q_LORA, KV_LORA, NOPE_C, ROPE_C, V_C = 8, 384, 256, 64, 32, 64
N_EXP, TOP_K, D_FF = 32, 4, 1024
SWIGLU_LIMIT, SWIGLU_ALPHA = 7.0, 1.702
GRID_W = 64
ROPE_BASE = 10000.0
EPS = 1e-6
NEG = -1e30
SCALE_A = HD_A ** -0.5
SCALE_C = (NOPE_C + ROPE_C) ** -0.5

LANES = 128
MOD_ROWS = 8
TM = 256
QB = 128
TME = 256
CONV_PAD = 16
CONV_ROWS = 64
VMEM_LIMIT = 56 * 1024 * 1024


def _cparams(sem):
    return pltpu.CompilerParams(dimension_semantics=sem, vmem_limit_bytes=VMEM_LIMIT)


def _sigmoid(x):
    return 1.0 / (1.0 + jnp.exp(-x))


def _dot(a, b):
    return jnp.dot(a, b, preferred_element_type=F32)


def _dot_nt(a, b):
    return lax.dot_general(a, b, (((1,), (1,)), ((), ())), preferred_element_type=F32)


def _rope_chunk(x, cos, sin, nfreq):
    lane = lax.broadcasted_iota(I32, x.shape, 1)
    first = (lane % (2 * nfreq)) < nfreq
    partner = jnp.where(first, pltpu.roll(x, LANES - nfreq, 1), pltpu.roll(x, nfreq, 1))
    return x * cos + partner * sin


def _lane_half_mask(shape, hi):
    lane = lax.broadcasted_iota(I32, shape, 1)
    return (lane >= LANES // 2) if hi else (lane < LANES // 2)


def _joint_softmax(parts, sink=None):
    m = None
    for s, _ in parts:
        ms = jnp.max(s, axis=-1, keepdims=True)
        m = ms if m is None else jnp.maximum(m, ms)
    if sink is not None:
        m = jnp.maximum(m, sink)
    den = None
    o = None
    for s, v in parts:
        e = jnp.exp(s - m)
        d = jnp.sum(e, axis=-1, keepdims=True)
        den = d if den is None else den + d
        ov = _dot(e.astype(BF16), v)
        o = ov if o is None else o + ov
    if sink is not None:
        den = den + jnp.exp(sink - m)
    return o / den


def _mod_kernel(c_ref, w_ref, b_ref, o_ref):
    c = c_ref[...]
    s = (c * _sigmoid(c)).astype(BF16)
    o_ref[0] = _dot(s, w_ref[0].astype(BF16)) + b_ref[0]


def _modulation(cond, mod_w, mod_b):
    L, D, D6 = mod_w.shape
    tn = 1024
    return pl.pallas_call(
        _mod_kernel,
        out_shape=jax.ShapeDtypeStruct((L, MOD_ROWS, D6), F32),
        grid=(L, D6 // tn),
        in_specs=[pl.BlockSpec((MOD_ROWS, D), lambda l, j: (0, 0)),
                  pl.BlockSpec((1, D, tn), lambda l, j: (l, 0, j)),
                  pl.BlockSpec((1, 1, tn), lambda l, j: (l, 0, j))],
        out_specs=pl.BlockSpec((1, MOD_ROWS, tn), lambda l, j: (l, 0, j)),
        compiler_params=_cparams(("arbitrary", "arbitrary")),
        name="modulation",
    )(cond, mod_w, mod_b.reshape(L, 1, D6))


_W_QA, _W_KA, _W_VA, _W_UB, _W_QC, _W_CKV, _W_KR, _W_GL = 0, 512, 640, 768, 1792, 2176, 2432, 2560
_W_COLS = 5632


def _in_proj_kernel(x_ref, mod_ref, g_ref, w_ref, qa_ref, ka_ref, va_ref, ub_ref, qc_ref, ckv_ref, kr_ref, gl_ref):
    D = x_ref.shape[1]
    x = x_ref[...]
    m = mod_ref[0]
    sh1, sc1 = m[:, 0:D], m[:, D:2 * D]
    y = x * lax.rsqrt(jnp.mean(x * x, axis=-1, keepdims=True) + EPS) * g_ref[...]
    h = (y * (1.0 + sc1) + sh1).astype(BF16)
    for ref, off in ((qa_ref, _W_QA), (ka_ref, _W_KA), (va_ref, _W_VA), (ub_ref, _W_UB), (qc_ref, _W_QC),
                     (ckv_ref, _W_CKV), (kr_ref, _W_KR), (gl_ref, _W_GL)):
        width = ref.shape[1]
        step = min(width, 512)
        for j in range(0, width, step):
            w = min(step, width - j)
            ref[:, j:j + w] = _dot(h, w_ref[:, off + j:off + j + w])


def _in_proj(x, mod3, g, w, mod_row):
    N, D = x.shape
    widths = (512, 128, 128, 1024, 384, 256, 128, 3072)
    tok = lambda i: (i, 0)
    return pl.pallas_call(
        _in_proj_kernel,
        out_shape=[jax.ShapeDtypeStruct((N, wd), F32) for wd in widths],
        grid=(N // TM,),
        in_specs=[pl.BlockSpec((TM, D), tok),
                  pl.BlockSpec((1, 1, mod3.shape[2]), lambda i: (mod_row(i), 0, 0)),
                  pl.BlockSpec((1, D), lambda i: (0, 0)),
                  pl.BlockSpec((D, _W_COLS), lambda i: (0, 0))],
        out_specs=[pl.BlockSpec((TM, wd), tok) for wd in widths],
        compiler_params=_cparams(("arbitrary",)),
        name="in_proj",
    )(x, mod3, g, w)


def _attn_a_ctx_kernel(sink_ref, q_ref, k_ref, v_ref, o_ref):
    kb = k_ref[...].astype(BF16)
    vb = v_ref[...].astype(BF16)
    for c in range(q_ref.shape[1] // LANES):
        q = q_ref[:, c * LANES:(c + 1) * LANES] * SCALE_A
        outs = []
        for hi in (False, True):
            mask = _lane_half_mask(q.shape, hi)
            s = _dot_nt(jnp.where(mask, q, 0.0).astype(BF16), kb)
            outs.append(_joint_softmax([(s, vb)], sink_ref[2 * c + int(hi)]))
        o_ref[:, c * LANES:(c + 1) * LANES] = jnp.where(_lane_half_mask(q.shape, False), outs[0], outs[1])


def _attn_a_ctx(sink, qa, ka, va, nb, S):
    return pl.pallas_call(
        _attn_a_ctx_kernel,
        out_shape=jax.ShapeDtypeStruct((nb * S, qa.shape[1]), F32),
        grid=(nb,),
        in_specs=[pl.BlockSpec(memory_space=pltpu.SMEM),
                  pl.BlockSpec((S, qa.shape[1]), lambda b: (b, 0)),
                  pl.BlockSpec((S, LANES), lambda b: (b, 0)),
                  pl.BlockSpec((S, LANES), lambda b: (b, 0))],
        out_specs=pl.BlockSpec((S, qa.shape[1]), lambda b: (b, 0)),
        compiler_params=_cparams(("arbitrary",)),
        name="attn_a_ctx",
    )(sink, qa, ka, va)


def _attn_a_lat_kernel(sink_ref, q_ref, k_ref, v_ref, kc_ref, vc_ref, cos_ref, sin_ref, o_ref):
    i = pl.program_id(1)
    T = k_ref.shape[0]
    r0 = pl.multiple_of(i * QB, QB)
    kb = _rope_chunk(k_ref[...], cos_ref[...], sin_ref[...], HD_A // 4).astype(BF16)
    vb = v_ref[...].astype(BF16)
    kcb = kc_ref[...].astype(BF16)
    vcb = vc_ref[...].astype(BF16)
    cq = cos_ref[pl.ds(r0, QB), :]
    sq = sin_ref[pl.ds(r0, QB), :]
    qpos = r0 + lax.broadcasted_iota(I32, (QB, T), 0)
    kpos = lax.broadcasted_iota(I32, (QB, T), 1)
    valid = jnp.abs(qpos - kpos) <= WINDOW
    for c in range(q_ref.shape[1] // LANES):
        q = _rope_chunk(q_ref[:, c * LANES:(c + 1) * LANES], cq, sq, HD_A // 4) * SCALE_A
        outs = []
        for hi in (False, True):
            qm = jnp.where(_lane_half_mask(q.shape, hi), q, 0.0).astype(BF16)
            s_loc = jnp.where(valid, _dot_nt(qm, kb), NEG)
            s_ctx = _dot_nt(qm, kcb)
            outs.append(_joint_softmax([(s_loc, vb), (s_ctx, vcb)], sink_ref[2 * c + int(hi)]))
        o_ref[:, c * LANES:(c + 1) * LANES] = jnp.where(_lane_half_mask(q.shape, False), outs[0], outs[1])


def _attn_a_lat(sink, qa, ka, va, cache_k, cache_v, cos, sin, layer, row0, nb, T):
    nq = T // QB
    past = cache_k.shape[2]
    return pl.pallas_call(
        _attn_a_lat_kernel,
        out_shape=jax.ShapeDtypeStruct((nb * T, qa.shape[1]), F32),
        grid=(nb, nq),
        in_specs=[pl.BlockSpec(memory_space=pltpu.SMEM),
                  pl.BlockSpec((QB, qa.shape[1]), lambda b, i: (row0 // QB + b * nq + i, 0)),
                  pl.BlockSpec((T, LANES), lambda b, i: (row0 // T + b, 0)),
                  pl.BlockSpec((T, LANES), lambda b, i: (row0 // T + b, 0)),
                  pl.BlockSpec((None, None, past, LANES), lambda b, i: (b, layer, 0, 0)),
                  pl.BlockSpec((None, None, past, LANES), lambda b, i: (b, layer, 0, 0)),
                  pl.BlockSpec((T, LANES), lambda b, i: (0, 0)),
                  pl.BlockSpec((T, LANES), lambda b, i: (0, 0))],
        out_specs=pl.BlockSpec((QB, qa.shape[1]), lambda b, i: (b * nq + i, 0)),
        compiler_params=_cparams(("arbitrary", "arbitrary")),
        name="attn_a_lat",
    )(sink, qa, ka, va, cache_k, cache_v, cos, sin)


def _conv_kernel(u_ref, w_ref, b_ref, g_ref, beta_ref, o_ref, gp_ref):
    T = u_ref.shape[0]
    C = o_ref.shape[1]
    u = u_ref[...]
    gp_ref[0:CONV_PAD, :] = jnp.zeros((CONV_PAD, C), F32)
    gp_ref[CONV_PAD + T:2 * CONV_PAD + T, :] = jnp.zeros((CONV_PAD, C), F32)
    gp_ref[CONV_PAD:CONV_PAD + T, :] = u[:, :C] * _sigmoid(u[:, C:])
    shift = CONV_PAD - CONV_W // 2

    def chunk(ci, carry):
        r0 = pl.multiple_of(ci * CONV_ROWS, CONV_ROWS)
        win = gp_ref[pl.ds(r0, CONV_ROWS + 2 * CONV_PAD), :]
        acc = jnp.broadcast_to(b_ref[...], (CONV_ROWS, C))
        for j in range(CONV_W):
            acc = acc + win[shift + j:shift + j + CONV_ROWS, :] * w_ref[j:j + 1, :]
        mu = jnp.mean(acc, axis=-1, keepdims=True)
        d = acc - mu
        var = jnp.mean(d * d, axis=-1, keepdims=True)
        z = d * lax.rsqrt(var + EPS) * g_ref[...] + beta_ref[...]
        o_ref[pl.ds(r0, CONV_ROWS), :] = z * _sigmoid(z)
        return carry

    lax.fori_loop(0, T // CONV_ROWS, chunk, 0)


def _conv_branch(ub, dw_w, dw_b, ln_g, ln_b, row0, nseq, T):
    vec = lambda s: (0, 0)
    return pl.pallas_call(
        _conv_kernel,
        out_shape=jax.ShapeDtypeStruct((nseq * T, C_B), F32),
        grid=(nseq,),
        in_specs=[pl.BlockSpec((T, 2 * C_B), lambda s: (row0 // T + s, 0)),
                  pl.BlockSpec((CONV_W, C_B), vec),
                  pl.BlockSpec((1, C_B), vec), pl.BlockSpec((1, C_B), vec), pl.BlockSpec((1, C_B), vec)],
        out_specs=pl.BlockSpec((T, C_B), lambda s: (s, 0)),
        scratch_shapes=[pltpu.VMEM((T + 2 * CONV_PAD, C_B), F32)],
        compiler_params=_cparams(("arbitrary",)),
        name="conv_branch",
    )(ub, dw_w, dw_b, ln_g, ln_b)


def _mla_proj_kernel(qc_ref, kvc_ref, kr_ref, qag_ref, kvg_ref, qbw_ref, kvbw_ref, cos_ref, sin_ref,
                     q_ref, ckv_ref, kro_ref, kv_ref):
    qc = qc_ref[...]
    qn = qc * lax.rsqrt(jnp.mean(qc * qc, axis=-1, keepdims=True) + EPS) * qag_ref[...]
    q = _dot(qn.astype(BF16), qbw_ref[...])
    nope = H_C * NOPE_C
    q_ref[:, 0:nope] = q[:, 0:nope]
    cos, sin = cos_ref[...], sin_ref[...]
    for c in range(H_C * ROPE_C // LANES):
        lo = nope + c * LANES
        q_ref[:, lo:lo + LANES] = _rope_chunk(q[:, lo:lo + LANES], cos, sin, ROPE_C // 4)
    kvc = kvc_ref[...]
    ckv = kvc * lax.rsqrt(jnp.mean(kvc * kvc, axis=-1, keepdims=True) + EPS) * kvg_ref[...]
    ckv_ref[...] = ckv
    kro_ref[...] = _rope_chunk(kr_ref[...], cos, sin, ROPE_C // 4)
    kv_ref[...] = _dot(ckv.astype(BF16), kvbw_ref[...])


def _mla_proj(qc, ckv_raw, kr_raw, qa_g, kv_g, qb_w, kvb_w, cos, sin):
    N = qc.shape[0]
    tok = lambda i: (i, 0)
    vec = lambda i: (0, 0)
    qw = H_C * (NOPE_C + ROPE_C)
    kvw = H_C * (NOPE_C + V_C)
    return pl.pallas_call(
        _mla_proj_kernel,
        out_shape=[jax.ShapeDtypeStruct((N, qw), F32), jax.ShapeDtypeStruct((N, KV_LORA), F32),
                   jax.ShapeDtypeStruct((N, LANES), F32), jax.ShapeDtypeStruct((N, kvw), F32)],
        grid=(N // TM,),
        in_specs=[pl.BlockSpec((TM, Q_LORA), tok), pl.BlockSpec((TM, KV_LORA), tok), pl.BlockSpec((TM, LANES), tok),
                  pl.BlockSpec((1, Q_LORA), vec), pl.BlockSpec((1, KV_LORA), vec),
                  pl.BlockSpec((Q_LORA, qw), vec), pl.BlockSpec((KV_LORA, kvw), vec),
                  pl.BlockSpec((TM, LANES), tok), pl.BlockSpec((TM, LANES), tok)],
        out_specs=[pl.BlockSpec((TM, qw), tok), pl.BlockSpec((TM, KV_LORA), tok),
                   pl.BlockSpec((TM, LANES), tok), pl.BlockSpec((TM, kvw), tok)],
        compiler_params=_cparams(("arbitrary",)),
        name="mla_proj",
    )(qc, ckv_raw, kr_raw, qa_g, kv_g, qb_w, kvb_w, cos, sin)


def _mla_heads(q, keysets, o_ref):
    nope = H_C * NOPE_C
    for c in range(nope // LANES):
        qn = q[:, c * LANES:(c + 1) * LANES]
        outs = []
        for hi in (False, True):
            h = 2 * c + int(hi)
            rc = (h * ROPE_C) // LANES
            qr = q[:, nope + rc * LANES:nope + (rc + 1) * LANES]
            lane = lax.broadcasted_iota(I32, qr.shape, 1)
            lo = (h * ROPE_C) % LANES
            qrm = jnp.where((lane >= lo) & (lane < lo + ROPE_C), qr, 0.0)
            qm = jnp.concatenate([jnp.where(_lane_half_mask(qn.shape, hi), qn, 0.0), qrm], axis=1).astype(BF16)
            parts = []
            for kn, kr4, v in keysets:
                kk = jnp.concatenate([kn[:, c * LANES:(c + 1) * LANES], kr4], axis=1)
                parts.append((_dot_nt(qm, kk) * SCALE_C, v[:, c * LANES:(c + 1) * LANES]))
            outs.append(_joint_softmax(parts))
        o_ref[:, c * LANES:(c + 1) * LANES] = jnp.where(_lane_half_mask(qn.shape, False), outs[0], outs[1])


def _mla_attn_ctx_kernel(q_ref, kr_ref, kv_ref, o_ref):
    nope = H_C * NOPE_C
    kv = kv_ref[...].astype(BF16)
    _mla_heads(q_ref[...], [(kv[:, :nope], kr_ref[...].astype(BF16), kv[:, nope:])], o_ref)


def _mla_attn_ctx(q, kr, kv, nb, S):
    return pl.pallas_call(
        _mla_attn_ctx_kernel,
        out_shape=jax.ShapeDtypeStruct((nb * S, H_C * V_C), F32),
        grid=(nb,),
        in_specs=[pl.BlockSpec((S, q.shape[1]), lambda b: (b, 0)),
                  pl.BlockSpec((S, LANES), lambda b: (b, 0)),
                  pl.BlockSpec((S, kv.shape[1]), lambda b: (b, 0))],
        out_specs=pl.BlockSpec((S, H_C * V_C), lambda b: (b, 0)),
        compiler_params=_cparams(("arbitrary",)),
        name="mla_attn_ctx",
    )(q, kr, kv)


def _mla_attn_lat_kernel(q_ref, kr_ref, kv_ref, cckv_ref, ckr_ref, kvbw_ref, o_ref, ckv_scr):
    nope = H_C * NOPE_C

    @pl.when(pl.program_id(1) == 0)
    def _():
        ckv_scr[...] = _dot(cckv_ref[...].astype(BF16), kvbw_ref[...]).astype(BF16)

    kv = kv_ref[...].astype(BF16)
    ckv = ckv_scr[...]
    _mla_heads(q_ref[...], [(kv[:, :nope], kr_ref[...].astype(BF16), kv[:, nope:]),
                            (ckv[:, :nope], ckr_ref[...].astype(BF16), ckv[:, nope:])], o_ref)


def _mla_attn_lat(q, kr, kv, cache_ckv, cache_kr4, kvb_w, layer, row0, nb, T):
    nq = T // QB
    past = cache_ckv.shape[2]
    kvw = kv.shape[1]
    return pl.pallas_call(
        _mla_attn_lat_kernel,
        out_shape=jax.ShapeDtypeStruct((nb * T, H_C * V_C), F32),
        grid=(nb, nq),
        in_specs=[pl.BlockSpec((QB, q.shape[1]), lambda b, i: (row0 // QB + b * nq + i, 0)),
                  pl.BlockSpec((T, LANES), lambda b, i: (row0 // T + b, 0)),
                  pl.BlockSpec((T, kvw), lambda b, i: (row0 // T + b, 0)),
                  pl.BlockSpec((None, None, past, KV_LORA), lambda b, i: (b, layer, 0, 0)),
                  pl.BlockSpec((None, None, past, LANES), lambda b, i: (b, layer, 0, 0)),
                  pl.BlockSpec((KV_LORA, kvw), lambda b, i: (0, 0))],
        out_specs=pl.BlockSpec((QB, H_C * V_C), lambda b, i: (b * nq + i, 0)),
        scratch_shapes=[pltpu.VMEM((past, kvw), BF16)],
        compiler_params=_cparams(("arbitrary", "arbitrary")),
        name="mla_attn_lat",
    )(q, kr, kv, cache_ckv, cache_kr4, kvb_w)


def _merge_kernel(oa_ref, cb_ref, oc_ref, gl_ref, x_ref, mod_ref, g2_ref, wa_ref, wb_ref, wc_ref, wo_ref,
                  rw_ref, rb_ref, x1_ref, h2_ref, pw_ref, ei_ref, rk_ref, cnt_ref, base_ref):
    D = x_ref.shape[1]
    i = pl.program_id(0)

    @pl.when(i == 0)
    def _():
        base_ref[...] = jnp.zeros(base_ref.shape, F32)

    m = mod_ref[0]
    gate1, sh2, sc2 = m[:, 2 * D:3 * D], m[:, 3 * D:4 * D], m[:, 4 * D:5 * D]
    merged = _sigmoid(gl_ref[:, 0:D]) * _dot(oa_ref[...].astype(BF16), wa_ref[...])
    merged = merged + _sigmoid(gl_ref[:, D:2 * D]) * _dot(cb_ref[...].astype(BF16), wb_ref[...])
    merged = merged + _sigmoid(gl_ref[:, 2 * D:3 * D]) * _dot(oc_ref[...].astype(BF16), wc_ref[...])
    x1 = x_ref[...] + gate1 * _dot(merged.astype(BF16), wo_ref[...])
    x1_ref[...] = x1
    y = x1 * lax.rsqrt(jnp.mean(x1 * x1, axis=-1, keepdims=True) + EPS) * g2_ref[...]
    h2 = y * (1.0 + sc2) + sh2
    h2_ref[...] = h2

    logits = jnp.dot(h2, rw_ref[...], preferred_element_type=F32, precision=lax.Precision.HIGHEST) + rb_ref[...]
    tm = logits.shape[0]
    lane = lax.broadcasted_iota(I32, logits.shape, 1).astype(F32)
    vals, idxs = [], []
    cur = logits
    for _ in range(TOP_K):
        mx = jnp.max(cur, axis=-1, keepdims=True)
        ix = jnp.min(jnp.where(cur == mx, lane, float(LANES)), axis=-1, keepdims=True)
        vals.append(mx)
        idxs.append(ix)
        cur = jnp.where(lane == ix, -jnp.inf, cur)
    es = [jnp.exp(v - vals[0]) for v in vals]
    den = es[0]
    for e in es[1:]:
        den = den + e
    sel = jnp.zeros(logits.shape, F32)
    for ix in idxs:
        sel = sel + jnp.where(lane == ix, 1.0, 0.0)
    row = lax.broadcasted_iota(I32, (tm, tm), 0)
    col = lax.broadcasted_iota(I32, (tm, tm), 1)
    lower = jnp.where(col < row, 1.0, 0.0).astype(BF16)
    before = _dot(lower, sel.astype(BF16)) + base_ref[...]
    pw = jnp.zeros(logits.shape, F32)
    ei = jnp.zeros(logits.shape, F32)
    rk = jnp.zeros(logits.shape, F32)
    for k in range(TOP_K):
        rank = jnp.sum(jnp.where(lane == idxs[k], before, 0.0), axis=-1, keepdims=True)
        slot = lane == float(k)
        pw = jnp.where(slot, es[k] / den, pw)
        ei = jnp.where(slot, idxs[k], ei)
        rk = jnp.where(slot, rank, rk)
    pw_ref[...] = pw
    ei_ref[...] = ei.astype(I32)
    rk_ref[...] = rk.astype(I32)
    total = base_ref[...] + jnp.sum(sel, axis=0, keepdims=True)
    base_ref[...] = total
    cnt_ref[...] = jnp.broadcast_to(total, cnt_ref.shape).astype(I32)


def _merge(oa, cb, oc, gl, x, mod3, g2, wa, wb, wc, wo, rw, rb, mod_row):
    N, D = x.shape
    tok = lambda i: (i, 0)
    vec = lambda i: (0, 0)
    full = lambda a: pl.BlockSpec(a.shape, vec)
    return pl.pallas_call(
        _merge_kernel,
        out_shape=[jax.ShapeDtypeStruct((N, D), F32), jax.ShapeDtypeStruct((N, D), F32),
                   jax.ShapeDtypeStruct((N, LANES), F32), jax.ShapeDtypeStruct((N, LANES), I32),
                   jax.ShapeDtypeStruct((N, LANES), I32), jax.ShapeDtypeStruct((8, LANES), I32)],
        grid=(N // TM,),
        in_specs=[pl.BlockSpec((TM, oa.shape[1]), tok), pl.BlockSpec((TM, cb.shape[1]), tok),
                  pl.BlockSpec((TM, oc.shape[1]), tok), pl.BlockSpec((TM, gl.shape[1]), tok),
                  pl.BlockSpec((TM, D), tok),
                  pl.BlockSpec((1, 1, mod3.shape[2]), lambda i: (mod_row(i), 0, 0)),
                  full(g2), full(wa), full(wb), full(wc), full(wo), full(rw), full(rb)],
        out_specs=[pl.BlockSpec((TM, D), tok), pl.BlockSpec((TM, D), tok), pl.BlockSpec((TM, LANES), tok),
                   pl.BlockSpec((TM, LANES), tok), pl.BlockSpec((TM, LANES), tok), pl.BlockSpec((8, LANES), vec)],
        scratch_shapes=[pltpu.VMEM((1, LANES), F32)],
        compiler_params=_cparams(("arbitrary",)),
        name="merge_router",
    )(oa, cb, oc, gl, x, mod3, g2, wa, wb, wc, wo, rw, rb)


def _row_copy(src_hbm, row, dst_ref, dst_row, sem):
    return pltpu.make_async_copy(src_hbm.at[pl.ds(row, 1), :], dst_ref.at[pl.ds(dst_row, 1), :], sem)


def _dispatch_kernel(src_ref, nt_ref, h_hbm, o_ref, sem):
    i = pl.program_id(0)
    rows = o_ref.shape[0]

    @pl.when(i < nt_ref[0])
    def _():
        def start(r, c):
            _row_copy(h_hbm, src_ref[i * rows + r], o_ref, r, sem).start()
            return c

        lax.fori_loop(0, rows, start, 0)

        def wait(r, c):
            _row_copy(h_hbm, 0, o_ref, r, sem).wait()
            return c

        lax.fori_loop(0, rows, wait, 0)

    @pl.when(i >= nt_ref[0])
    def _():
        o_ref[...] = jnp.zeros(o_ref.shape, o_ref.dtype)


def _dispatch(src, nt, h2, P):
    D = h2.shape[1]
    return pl.pallas_call(
        _dispatch_kernel,
        out_shape=jax.ShapeDtypeStruct((P, D), F32),
        grid_spec=pltpu.PrefetchScalarGridSpec(
            num_scalar_prefetch=2, grid=(P // TME,),
            in_specs=[pl.BlockSpec(memory_space=pl.ANY)],
            out_specs=pl.BlockSpec((TME, D), lambda i, s, n: (i, 0)),
            scratch_shapes=[pltpu.SemaphoreType.DMA]),
        compiler_params=_cparams(("arbitrary",)),
        name="moe_dispatch",
    )(src, nt, h2)


def _experts_kernel(te_ref, nt_ref, x_ref, w1_ref, b1_ref, w2_ref, b2_ref, o_ref, w1b_ref, w2b_ref):
    i = pl.program_id(0)
    valid = i < nt_ref[0]
    prev = te_ref[jnp.maximum(i - 1, 0)]
    fresh = jnp.logical_or(i == 0, te_ref[i] != prev)

    @pl.when(jnp.logical_and(valid, fresh))
    def _():
        w1b_ref[...] = w1_ref[...].astype(BF16)
        w2b_ref[...] = w2_ref[...].astype(BF16)

    @pl.when(valid)
    def _():
        u = _dot(x_ref[...].astype(BF16), w1b_ref[...]) + b1_ref[...]
        gl = jnp.minimum(u[:, :D_FF], SWIGLU_LIMIT)
        up = jnp.clip(u[:, D_FF:], -SWIGLU_LIMIT, SWIGLU_LIMIT)
        a = (up + 1.0) * (gl * _sigmoid(SWIGLU_ALPHA * gl))
        o_ref[...] = _dot(a.astype(BF16), w2b_ref[...]) + b2_ref[...]

    @pl.when(jnp.logical_not(valid))
    def _():
        o_ref[...] = jnp.zeros(o_ref.shape, o_ref.dtype)


def _experts(te, nt, xs, w1, b1, w2, b2, layer):
    P, D = xs.shape
    last = lambda i, n: jnp.minimum(i, jnp.maximum(n[0] - 1, 0))
    return pl.pallas_call(
        _experts_kernel,
        out_shape=jax.ShapeDtypeStruct((P, D), F32),
        grid_spec=pltpu.PrefetchScalarGridSpec(
            num_scalar_prefetch=2, grid=(P // TME,),
            in_specs=[pl.BlockSpec((TME, D), lambda i, t, n: (last(i, n), 0)),
                      pl.BlockSpec((None, None, D, 2 * D_FF), lambda i, t, n: (layer, t[last(i, n)], 0, 0)),
                      pl.BlockSpec((None, None, 1, 2 * D_FF), lambda i, t, n: (layer, t[last(i, n)], 0, 0)),
                      pl.BlockSpec((None, None, D_FF, D), lambda i, t, n: (layer, t[last(i, n)], 0, 0)),
                      pl.BlockSpec((None, None, 1, D), lambda i, t, n: (layer, t[last(i, n)], 0, 0))],
            out_specs=pl.BlockSpec((TME, D), lambda i, t, n: (i, 0)),
            scratch_shapes=[pltpu.VMEM((D, 2 * D_FF), BF16), pltpu.VMEM((D_FF, D), BF16)]),
        compiler_params=_cparams(("arbitrary",)),
        name="moe_experts",
    )(te, nt, xs, w1, b1, w2, b2)


def _combine_kernel(pos_ref, y_hbm, x1_ref, pw_ref, mod_ref, o_ref, buf_ref, sem):
    i = pl.program_id(0)
    tm, D = x1_ref.shape

    def start(t, c):
        for k in range(TOP_K):
            _row_copy(y_hbm, pos_ref[(i * tm + t) * TOP_K + k], buf_ref.at[k], t, sem).start()
        return c

    lax.fori_loop(0, tm, start, 0)

    def wait(t, c):
        for k in range(TOP_K):
            _row_copy(y_hbm, 0, buf_ref.at[k], t, sem).wait()
        return c

    lax.fori_loop(0, tm, wait, 0)
    pw = pw_ref[...]
    acc = pw[:, 0:1] * buf_ref[0]
    for k in range(1, TOP_K):
        acc = acc + pw[:, k:k + 1] * buf_ref[k]
    gate2 = mod_ref[0][:, 5 * D:6 * D]
    o_ref[...] = x1_ref[...] + gate2 * acc


def _combine(pos, ys, x1, pw, mod3, mod_row):
    N, D = x1.shape
    return pl.pallas_call(
        _combine_kernel,
        out_shape=jax.ShapeDtypeStruct((N, D), F32),
        grid_spec=pltpu.PrefetchScalarGridSpec(
            num_scalar_prefetch=1, grid=(N // TM,),
            in_specs=[pl.BlockSpec(memory_space=pl.ANY),
                      pl.BlockSpec((TM, D), lambda i, p: (i, 0)),
                      pl.BlockSpec((TM, LANES), lambda i, p: (i, 0)),
                      pl.BlockSpec((1, 1, mod3.shape[2]), lambda i, p: (mod_row(i), 0, 0))],
            out_specs=pl.BlockSpec((TM, D), lambda i, p: (i, 0)),
            scratch_shapes=[pltpu.VMEM((TOP_K, TM, D), F32), pltpu.SemaphoreType.DMA]),
        compiler_params=_cparams(("arbitrary",)),
        name="moe_combine",
    )(pos, ys, x1, pw, mod3)


def _final_norm_kernel(x_ref, g_ref, o_ref):
    x = x_ref[...]
    o_ref[...] = x * lax.rsqrt(jnp.mean(x * x, axis=-1, keepdims=True) + EPS) * g_ref[...]


def _final_norm(x, g):
    N, D = x.shape
    return pl.pallas_call(
        _final_norm_kernel,
        out_shape=jax.ShapeDtypeStruct((N, D), F32),
        grid=(N // TM,),
        in_specs=[pl.BlockSpec((TM, D), lambda i: (i, 0)), pl.BlockSpec((1, D), lambda i: (0, 0))],
        out_specs=pl.BlockSpec((TM, D), lambda i: (i, 0)),
        compiler_params=_cparams(("arbitrary",)),
        name="final_norm",
    )(x, g)


def _rope_tables(T, d_head):
    half = d_head // 2
    inv = ROPE_BASE ** (-jnp.arange(0, half, 2, dtype=F32) / half)
    t = jnp.arange(T)
    ang_r = (t // GRID_W).astype(F32)[:, None] * inv[None, :]
    ang_c = (t % GRID_W).astype(F32)[:, None] * inv[None, :]
    cos = jnp.concatenate([jnp.cos(ang_r), jnp.cos(ang_r), jnp.cos(ang_c), jnp.cos(ang_c)], axis=1)
    sin = jnp.concatenate([-jnp.sin(ang_r), jnp.sin(ang_r), -jnp.sin(ang_c), jnp.sin(ang_c)], axis=1)
    reps = LANES // d_head
    return jnp.tile(cos, (1, reps)), jnp.tile(sin, (1, reps))


def _route(ei, rk, cnt, N, P):
    e = ei[:, :TOP_K]
    counts = cnt[0, :N_EXP]
    sizes = ((counts + TME - 1) // TME) * TME
    ends = jnp.cumsum(sizes)
    starts = ends - sizes
    pos = (jnp.take(starts, e) + rk[:, :TOP_K]).reshape(-1).astype(I32)
    tok = jnp.repeat(jnp.arange(N, dtype=I32), TOP_K)
    src = jnp.zeros((P,), I32).at[pos].set(tok)
    nt = (ends[-1] // TME).astype(I32).reshape(1)
    te = jnp.searchsorted(ends // TME, jnp.arange(P // TME, dtype=I32), side='right')
    te = jnp.minimum(te, N_EXP - 1).astype(I32)
    return pos, src, nt, te


def kernel(x_prompt, x_sample, cache_swa_k, cache_swa_v, cache_mla_ckv, cache_mla_krope, c, c_ctx, mod_w, mod_b, norm1_g, norm2_g, w_in, swa_sink, conv_dw_w, conv_dw_b, conv_ln_g, conv_ln_b, mla_qa_g, mla_qb_w, mla_kv_g, mla_kvb_w, w_br_a, w_br_b, w_br_c, w_out, router_w, router_b, exp_w1, exp_b1, exp_w2, exp_b2, final_g):
    B, S, D = x_prompt.shape
    DB, T, _ = x_sample.shape
    L = mod_w.shape[0]
    past = cache_swa_k.shape[2]
    NC, NL = B * S, DB * T
    N = NC + NL
    assert 1 + DB <= MOD_ROWS and S % TM == 0 and T % TM == 0 and T % GRID_W == 0
    assert NC % T == 0 and S >= 2 * CONV_PAD
    P = N * TOP_K + N_EXP * TME
    nct, tpl = NC // TM, T // TM

    def mod_row(i):
        return jnp.where(i < nct, 0, 1 + (i - nct) // tpl)

    qa_perm = jnp.concatenate([jnp.arange(HD_A) + HD_A * h for cc in range(H_A // 2) for h in (cc, H_A // 2 + cc)])
    wq = jnp.take(w_in[:, :, :H_A * HD_A], qa_perm, axis=2)
    o_kr = H_A * HD_A + 2 * KV_A * HD_A + 2 * C_B + Q_LORA + KV_LORA
    w_kr = jnp.tile(w_in[:, :, o_kr:o_kr + ROPE_C], (1, 1, LANES // ROPE_C))
    w_pack = jnp.concatenate([wq, w_in[:, :, H_A * HD_A:o_kr], w_kr, w_in[:, :, o_kr + ROPE_C:]], axis=2).astype(BF16)
    assert w_pack.shape[2] == _W_COLS
    wa = jnp.take(w_br_a, qa_perm, axis=1).astype(BF16)
    wb, wc, wo = w_br_b.astype(BF16), w_br_c.astype(BF16), w_out.astype(BF16)
    hq = NOPE_C + ROPE_C
    qb_perm = jnp.concatenate([jnp.arange(NOPE_C) + hq * h for h in range(H_C)]
                              + [jnp.arange(ROPE_C) + hq * h + NOPE_C for h in range(H_C)])
    qbw = jnp.take(mla_qb_w, qb_perm, axis=2).astype(BF16)
    hk = NOPE_C + V_C
    kvb_perm = jnp.concatenate([jnp.arange(NOPE_C) + hk * h for h in range(H_C)]
                               + [jnp.arange(V_C) + hk * h + NOPE_C for h in range(H_C)])
    kvbw = jnp.take(mla_kvb_w, kvb_perm, axis=2).astype(BF16)
    sink = jnp.take(swa_sink, jnp.array([h for cc in range(H_A // 2) for h in (cc, H_A // 2 + cc)]), axis=1)
    rw = jnp.pad(router_w, ((0, 0), (0, 0), (0, LANES - N_EXP)))
    rb = jnp.pad(router_b, ((0, 0), (0, LANES - N_EXP)), constant_values=NEG)

    cos_a, sin_a = _rope_tables(T, HD_A)
    cos_c, sin_c = _rope_tables(T, ROPE_C)
    cos_cn = jnp.concatenate([jnp.ones((NC, LANES), F32), jnp.tile(cos_c, (DB, 1))], axis=0)
    sin_cn = jnp.concatenate([jnp.zeros((NC, LANES), F32), jnp.tile(sin_c, (DB, 1))], axis=0)
    cache_k = cache_swa_k.reshape(DB, L, past, KV_A * HD_A)
    cache_v = cache_swa_v.reshape(DB, L, past, KV_A * HD_A)
    cache_kr4 = jnp.tile(cache_mla_krope, (1, 1, 1, LANES // ROPE_C))

    cond = jnp.zeros((MOD_ROWS, D), F32).at[0].set(c_ctx).at[1:1 + DB].set(c)
    mod = _modulation(cond, mod_w, mod_b)

    x = jnp.concatenate([x_prompt.reshape(NC, D), x_sample.reshape(NL, D)], axis=0)
    ks_a, vs_a, ckvs, krs = [], [], [], []
    for l in range(L):
        mod3 = mod[l].reshape(MOD_ROWS, 1, 6 * D)
        qa, ka, va, ub, qc, ckv_raw, kr_raw, gl = _in_proj(x, mod3, norm1_g[l][None], w_pack[l], mod_row)
        oa = jnp.concatenate([
            _attn_a_ctx(sink[l], qa, ka, va, B, S),
            _attn_a_lat(sink[l], qa, ka, va, cache_k, cache_v, cos_a, sin_a, l, NC, DB, T)], axis=0)
        cb = jnp.concatenate([
            _conv_branch(ub, conv_dw_w[l], conv_dw_b[l][None], conv_ln_g[l][None], conv_ln_b[l][None], 0, B, S),
            _conv_branch(ub, conv_dw_w[l], conv_dw_b[l][None], conv_ln_g[l][None], conv_ln_b[l][None], NC, DB, T)],
            axis=0)
        q, ckv, kr, kv = _mla_proj(qc, ckv_raw, kr_raw, mla_qa_g[l][None], mla_kv_g[l][None], qbw[l], kvbw[l],
                                   cos_cn, sin_cn)
        oc = jnp.concatenate([
            _mla_attn_ctx(q, kr, kv, B, S),
            _mla_attn_lat(q, kr, kv, cache_mla_ckv, cache_kr4, kvbw[l], l, NC, DB, T)], axis=0)
        ks_a.append(ka[:NC].reshape(B, S, KV_A, HD_A))
        vs_a.append(va[:NC].reshape(B, S, KV_A, HD_A))
        ckvs.append(ckv[:NC].reshape(B, S, KV_LORA))
        krs.append(kr[:NC, :ROPE_C].reshape(B, S, ROPE_C))

        x1, h2, pw, ei, rk, cnt = _merge(oa, cb, oc, gl, x, mod3, norm2_g[l][None], wa[l], wb[l], wc[l], wo[l],
                                         rw[l], rb[l][None], mod_row)
        pos, src, nt, te = _route(ei, rk, cnt, N, P)
        xs = _dispatch(src, nt, h2, P)
        ys = _experts(te, nt, xs, exp_w1, exp_b1.reshape(L, N_EXP, 1, 2 * D_FF), exp_w2,
                      exp_b2.reshape(L, N_EXP, 1, D), l)
        x = _combine(pos, ys, x1, pw, mod3, mod_row)

    y = _final_norm(x, final_g[None])
    return (y[:NC].reshape(B, S, D), y[NC:].reshape(DB, T, D),
            jnp.stack(ks_a, axis=1), jnp.stack(vs_a, axis=1), jnp.stack(ckvs, axis=1), jnp.stack(krs, axis=1))
```
